```python
import jax, jax.numpy as jnp
from jax import lax
import numpy as np

D_MODEL = 2048
BATCH = 4
SEQ = 4096
DEPTH = 2

HEAD_DIM = 128
MIX_WIDTH = D_MODEL
MEM_WIDTH = D_MODEL // 4
N_MEM_HEADS = MEM_WIDTH // HEAD_DIM
N_MEM = 256
LRU_WIDTH = MIX_WIDTH - MEM_WIDTH
N_LRU_BLOCKS = LRU_WIDTH // HEAD_DIM
FOX_WIDTH = MIX_WIDTH - MEM_WIDTH
N_FOX_HEADS = FOX_WIDTH // HEAD_DIM
CONV_WIDTH = 4
LRU_C = 8.0
BLOCK_Q = 128
PEER_HEADS = 8
N_KEYS = 128
N_EXPERTS = N_KEYS * N_KEYS
PEER_TOPK = 16
D_QUERY = 256
PEER_HALF = D_QUERY // 2
PEER_CHUNK = 128
N_A_LAYERS = (DEPTH + 1) // 2
N_B_LAYERS = DEPTH // 2
RMS_EPS = 1e-6

kernel_name = "yoco_rglru_fox_peer_memory"


def rmsnorm(x, g):
    xf = x.astype(jnp.float32)
    y = xf * lax.rsqrt(jnp.mean(xf * xf, axis=-1, keepdims=True) + RMS_EPS)
    return (y * g.astype(jnp.float32)).astype(x.dtype)


def causal_conv(x, w, b):
    c = x.shape[-1]
    y = lax.conv_general_dilated(x, w[:, None, :], window_strides=(1,), padding=((CONV_WIDTH - 1, 0),),
                                 dimension_numbers=('NWC', 'WIO', 'NWC'), feature_group_count=c)
    return y + b


def rg_lru(x, gate_w, gate_b, lam):
    b_, s_, c_ = x.shape
    xb = x.reshape(b_, s_, N_LRU_BLOCKS, HEAD_DIM)
    gates = (jnp.einsum('bsnc,ncg->bsng', xb, gate_w) + gate_b).astype(jnp.float32)
    r = jax.nn.sigmoid(gates[..., :HEAD_DIM]).reshape(b_, s_, c_)
    i = jax.nn.sigmoid(gates[..., HEAD_DIM:]).reshape(b_, s_, c_)
    log_a = -LRU_C * r * jax.nn.softplus(-lam.astype(jnp.float32))
    a = jnp.exp(log_a)
    u = jnp.sqrt(-jnp.expm1(2.0 * log_a)) * (i * x.astype(jnp.float32))

    def combine(left, right):
        a1, b1 = left
        a2, b2 = right
        return a1 * a2, a2 * b1 + b2

    _, h = lax.associative_scan(combine, (a, u), axis=1)
    return h.astype(x.dtype)


def mem_attention(qm, mem_n, w_kv, q_g, k_g):
    b_, s_, _ = qm.shape
    kv = mem_n @ w_kv
    k = rmsnorm(kv[..., :MEM_WIDTH].reshape(b_, N_MEM, N_MEM_HEADS, HEAD_DIM), k_g)
    v = kv[..., MEM_WIDTH:].reshape(b_, N_MEM, N_MEM_HEADS, HEAD_DIM)
    q = rmsnorm(qm.reshape(b_, s_, N_MEM_HEADS, HEAD_DIM), q_g)
    s = jnp.einsum('bshd,bmhd->bhsm', q, k).astype(jnp.float32) * (HEAD_DIM ** -0.5)
    p = jax.nn.softmax(s, axis=-1)
    o = jnp.einsum('bhsm,bmhd->bshd', p.astype(v.dtype), v)
    return o.reshape(b_, s_, MEM_WIDTH)


def shared_kv(x, norm_g, w_kvf, b_f, k_g):
    b_, s_, _ = x.shape
    z = rmsnorm(x, norm_g) @ w_kvf
    k = rmsnorm(z[..., :FOX_WIDTH].reshape(b_, s_, N_FOX_HEADS, HEAD_DIM), k_g)
    v = z[..., FOX_WIDTH:2 * FOX_WIDTH].reshape(b_, s_, N_FOX_HEADS, HEAD_DIM)
    log_f = jax.nn.log_sigmoid((z[..., 2 * FOX_WIDTH:] + b_f).astype(jnp.float32))
    c = jnp.cumsum(log_f, axis=1).transpose(0, 2, 1)
    return k, v, c


def forgetting_attention(q, k, v, c):
    b_, s_, h_, d_ = q.shape
    n_blocks = s_ // BLOCK_Q
    qb = q.reshape(b_, n_blocks, BLOCK_Q, h_, d_).transpose(1, 0, 2, 3, 4)
    cb = c.reshape(b_, h_, n_blocks, BLOCK_Q).transpose(2, 0, 1, 3)
    key_pos = jnp.arange(s_)

    def one_block(args):
        qi, ci, blk = args
        s = jnp.einsum('bqhd,bkhd->bhqk', qi, k).astype(jnp.float32) * (HEAD_DIM ** -0.5)
        s = s + ci[..., :, None] - c[:, :, None, :]
        q_pos = blk * BLOCK_Q + jnp.arange(BLOCK_Q)
        mask = key_pos[None, :] <= q_pos[:, None]
        s = jnp.where(mask, s, -jnp.inf)
        p = jax.nn.softmax(s, axis=-1)
        return jnp.einsum('bhqk,bkhd->bqhd', p.astype(v.dtype), v)

    out = lax.map(one_block, (qb, cb, jnp.arange(n_blocks)))
    return out.transpose(1, 0, 2, 3, 4).reshape(b_, s_, h_ * d_)


def peer(h, w_q, subkeys, u, v):
    b_, s_, d_ = h.shape
    t_ = b_ * s_
    hf = h.reshape(t_, d_)
    q = (hf @ w_q).reshape(t_, PEER_HEADS, 2, PEER_HALF)
    scores = jnp.einsum('thpc,hpkc->thpk', q, subkeys).astype(jnp.float32)
    top_s, top_i = lax.top_k(scores, PEER_TOPK)
    cand_s = top_s[:, :, 0, :, None] + top_s[:, :, 1, None, :]
    cand_i = top_i[:, :, 0, :, None] * N_KEYS + top_i[:, :, 1, None, :]
    best_s, best_j = lax.top_k(cand_s.reshape(t_, PEER_HEADS, PEER_TOPK * PEER_TOPK), PEER_TOPK)
    idx = jnp.take_along_axis(cand_i.reshape(t_, PEER_HEADS, PEER_TOPK * PEER_TOPK), best_j, axis=-1)
    g = jax.nn.softmax(best_s, axis=-1).astype(h.dtype)
    n_sel = PEER_HEADS * PEER_TOPK
    n_chunks = t_ // PEER_CHUNK

    def one_chunk(args):
        xc, ic, gc = args
        act = jax.nn.gelu(jnp.einsum('ced,cd->ce', u[ic], xc))
        return jnp.einsum('ce,ced->cd', gc * act, v[ic])

    out = lax.map(one_chunk, (hf.reshape(n_chunks, PEER_CHUNK, d_),
                              idx.reshape(n_chunks, PEER_CHUNK, n_sel),
                              g.reshape(n_chunks, PEER_CHUNK, n_sel)))
    return out.reshape(b_, s_, d_)


def setup_inputs(seed: int = 0) -> dict:
    key = jax.random.key(seed)
    ks = jax.random.split(key, 32)
    f32 = jnp.float32
    nrm = lambda k, shape, scale: jax.random.normal(k, shape, f32) * scale
    gain = lambda k, shape: 1.0 + 0.02 * jax.random.normal(k, shape, f32)
    a0 = jax.random.uniform(ks[8], (N_A_LAYERS, LRU_WIDTH), f32, 0.9, 0.999) ** (1.0 / LRU_C)
    return {
        "x": nrm(ks[0], (BATCH, SEQ, D_MODEL), 1.0),
        "mem": nrm(ks[1], (BATCH, N_MEM, D_MODEL), 1.0),
        "a_norm_g": gain(ks[2], (N_A_LAYERS, D_MODEL)),
        "a_w_in": nrm(ks[3], (N_A_LAYERS, D_MODEL, 2 * LRU_WIDTH + MEM_WIDTH), D_MODEL ** -0.5),
        "a_conv_w": nrm(ks[4], (N_A_LAYERS, CONV_WIDTH, LRU_WIDTH), CONV_WIDTH ** -0.5),
        "a_conv_b": nrm(ks[5], (N_A_LAYERS, LRU_WIDTH), 0.02),
        "a_gate_w": nrm(ks[6], (N_A_LAYERS, N_LRU_BLOCKS, HEAD_DIM, 2 * HEAD_DIM), HEAD_DIM ** -0.5),
        "a_gate_b": nrm(ks[7], (N_A_LAYERS, N_LRU_BLOCKS, 2 * HEAD_DIM), 0.02),
        "a_lambda": jnp.log(a0) - jnp.log1p(-a0),
        "a_w_out": nrm(ks[9], (N_A_LAYERS, MIX_WIDTH, D_MODEL), MIX_WIDTH ** -0.5),
        "s_norm_g": gain(ks[10], (D_MODEL,)),
        "s_w_kvf": nrm(ks[11], (D_MODEL, 2 * FOX_WIDTH + N_FOX_HEADS), D_MODEL ** -0.5),
        "s_b_f": 2.0 + 0.5 * jax.random.normal(ks[12], (N_FOX_HEADS,), f32),
        "s_k_norm_g": gain(ks[13], (HEAD_DIM,)),
        "b_norm_g": gain(ks[14], (N_B_LAYERS, D_MODEL)),
        "b_w_in": nrm(ks[15], (N_B_LAYERS, D_MODEL, FOX_WIDTH + MEM_WIDTH), D_MODEL ** -0.5),
        "b_q_norm_g": gain(ks[16], (N_B_LAYERS, HEAD_DIM)),
        "b_w_out": nrm(ks[17], (N_B_LAYERS, MIX_WIDTH, D_MODEL), MIX_WIDTH ** -0.5),
        "m_norm_g": gain(ks[18], (DEPTH, D_MODEL)),
        "m_w_kv": nrm(ks[19], (DEPTH, D_MODEL, 2 * MEM_WIDTH), D_MODEL ** -0.5),
        "m_q_norm_g": gain(ks[20], (DEPTH, HEAD_DIM)),
        "m_k_norm_g": gain(ks[21], (DEPTH, HEAD_DIM)),
        "p_norm_g": gain(ks[22], (DEPTH, D_MODEL)),
        "p_w_q": nrm(ks[23], (DEPTH, D_MODEL, PEER_HEADS * D_QUERY), D_MODEL ** -0.5),
        "p_subkeys": nrm(ks[24], (DEPTH, PEER_HEADS, 2, N_KEYS, PEER_HALF), PEER_HALF ** -0.5),
        "p_u": nrm(ks[25], (DEPTH, N_EXPERTS, D_MODEL), D_MODEL ** -0.5),
        "p_v": nrm(ks[26], (DEPTH, N_EXPERTS, D_MODEL), (PEER_HEADS * PEER_TOPK) ** -0.5),
    }


def reference(x, mem, a_norm_g, a_w_in, a_conv_w, a_conv_b, a_gate_w, a_gate_b, a_lambda, a_w_out,
              s_norm_g, s_w_kvf, s_b_f, s_k_norm_g, b_norm_g, b_w_in, b_q_norm_g, b_w_out,
              m_norm_g, m_w_kv, m_q_norm_g, m_k_norm_g, p_norm_g, p_w_q, p_subkeys, p_u, p_v):
    b_, s_, _ = x.shape
    k_sh = v_sh = c_sh = None
    for layer in range(DEPTH):
        mem_n = rmsnorm(mem, m_norm_g[layer])
        if layer < N_A_LAYERS:
            i = layer
            z = rmsnorm(x, a_norm_g[i]) @ a_w_in[i]
            xb = causal_conv(z[..., :LRU_WIDTH], a_conv_w[i], a_conv_b[i])
            yb = z[..., LRU_WIDTH:2 * LRU_WIDTH]
            qm = z[..., 2 * LRU_WIDTH:]
            main = rg_lru(xb, a_gate_w[i], a_gate_b[i], a_lambda[i]) * jax.nn.gelu(yb)
            w_out = a_w_out[i]
        else:
            if layer == N_A_LAYERS:
                k_sh, v_sh, c_sh = shared_kv(x, s_norm_g, s_w_kvf, s_b_f, s_k_norm_g)
            j = layer - N_A_LAYERS
            z = rmsnorm(x, b_norm_g[j]) @ b_w_in[j]
            q = rmsnorm(z[..., :FOX_WIDTH].reshape(b_, s_, N_FOX_HEADS, HEAD_DIM), b_q_norm_g[j])
            qm = z[..., FOX_WIDTH:]
            main = forgetting_attention(q, k_sh, v_sh, c_sh)
            w_out = b_w_out[j]
        mo = mem_attention(qm, mem_n, m_w_kv[layer], m_q_norm_g[layer], m_k_norm_g[layer])
        x = x + jnp.concatenate([main, mo], axis=-1) @ w_out
        x = x + peer(rmsnorm(x, p_norm_g[layer]), p_w_q[layer], p_subkeys[layer], p_u[layer], p_v[layer])
    return x
```

```python
import functools

import jax
import jax.numpy as jnp
from jax import lax
from jax.experimental import pallas as pl
from jax.experimental.pallas import tpu as pltpu

F32 = jnp.float32
BF16 = jnp.bfloat16

HEAD_DIM = 128
LANES = 128
SUBLANES = 8
CONV_WIDTH = 4
LRU_C = 8.0
N_KEYS = 128
PEER_HEADS = 8
PEER_TOPK = 16
RMS_EPS = 1e-6
VMEM_LIMIT_BYTES = 48 * 1024 * 1024


def _params(semantics):
    return pltpu.CompilerParams(dimension_semantics=semantics, vmem_limit_bytes=VMEM_LIMIT_BYTES)


def _head_rmsnorm(y, gain):
    ms = jnp.mean(y * y, axis=-1, keepdims=True)
    return (y * lax.rsqrt(ms + RMS_EPS)) * gain


def _norm_matmul_kernel(x_ref, g_ref, w_ref, hg_ref, o_ref, xn_ref, *, head_lo, head_hi):
    j = pl.program_id(1)

    @pl.when(j == 0)
    def _():
        x = x_ref[...]
        ms = jnp.mean(x * x, axis=-1, keepdims=True)
        xn_ref[...] = ((x * lax.rsqrt(ms + RMS_EPS)) * g_ref[...]).astype(BF16)

    y = jnp.dot(xn_ref[...], w_ref[...], preferred_element_type=F32)
    tn = y.shape[1]

    def store_normed():
        for c in range(tn // HEAD_DIM):
            sl = slice(c * HEAD_DIM, (c + 1) * HEAD_DIM)
            o_ref[:, sl] = _head_rmsnorm(y[:, sl], hg_ref[:, sl]).astype(o_ref.dtype)

    def store_plain():
        o_ref[...] = y.astype(o_ref.dtype)

    if head_hi <= head_lo:
        store_plain()
    else:
        normed = jnp.logical_and(j >= head_lo, j < head_hi)
        pl.when(normed)(store_normed)
        pl.when(jnp.logical_not(normed))(store_plain)


def norm_matmul(x, g, w, *, head_gain=None, head_tiles=(0, 0), out_dtype=F32, tm=1024, tn=512):
    m, d = x.shape
    n = w.shape[1]
    tm = min(tm, m)
    tn = min(tn, n)
    assert m % tm == 0 and n % tn == 0 and tn % HEAD_DIM == 0
    if head_gain is None:
        head_gain = jnp.ones((n,), F32)
    return pl.pallas_call(
        functools.partial(_norm_matmul_kernel, head_lo=head_tiles[0], head_hi=head_tiles[1]),
        grid=(m // tm, n // tn),
        in_specs=[
            pl.BlockSpec((tm, d), lambda i, j: (i, 0)),
            pl.BlockSpec((1, d), lambda i, j: (0, 0)),
            pl.BlockSpec((d, tn), lambda i, j: (0, j)),
            pl.BlockSpec((1, tn), lambda i, j: (0, j)),
        ],
        out_specs=pl.BlockSpec((tm, tn), lambda i, j: (i, j)),
        out_shape=jax.ShapeDtypeStruct((m, n), out_dtype),
        scratch_shapes=[pltpu.VMEM((tm, d), BF16)],
        compiler_params=_params(("parallel", "arbitrary")),
        name="norm_matmul",
    )(x, g.reshape(1, d), w.astype(BF16), head_gain.reshape(1, n))


def _scan_shift(v, k, row, fill):
    return jnp.where(row >= k, pltpu.roll(v, k, 0), fill)


def _lru_kernel(x_ref, y_ref, cw_ref, cb_ref, gw_ref, gb_ref, lam_ref, o_ref,
                xe_ref, a_ref, u_ref, h_ref, *, ts, c):
    s = pl.program_id(1)

    @pl.when(s == 0)
    def _():
        xe_ref[0:SUBLANES, :] = jnp.zeros((SUBLANES, c), F32)
        h_ref[...] = jnp.zeros((SUBLANES, c), F32)

    x = x_ref[0]
    xe_ref[SUBLANES:SUBLANES + ts, :] = x
    cw = cw_ref[...]
    xc = cw[3:4, :] * x
    for k in range(CONV_WIDTH - 1):
        off = SUBLANES - (CONV_WIDTH - 1) + k
        xc = xc + cw[k:k + 1, :] * xe_ref[off:off + ts, :]
    xc = xc + cb_ref[...]
    xe_ref[0:SUBLANES, :] = x[ts - SUBLANES:ts, :]

    lam = lam_ref[...]
    softplus_neg_lam = jnp.maximum(-lam, 0.0) + jnp.log1p(jnp.exp(-jnp.abs(lam)))
    xcb = xc.astype(BF16)
    for n in range(c // HEAD_DIM):
        sl = slice(n * HEAD_DIM, (n + 1) * HEAD_DIM)
        gates = jnp.dot(xcb[:, sl], gw_ref[n], preferred_element_type=F32) + gb_ref[n]
        r = jax.nn.sigmoid(gates[:, :HEAD_DIM])
        i = jax.nn.sigmoid(gates[:, HEAD_DIM:])
        log_a = (-LRU_C * r) * softplus_neg_lam[:, sl]
        a = jnp.exp(log_a)
        a_ref[:, sl] = a
        u_ref[:, sl] = jnp.sqrt(-jnp.tanh(log_a) * (a * a + 1.0)) * (i * xc[:, sl])

    row = lax.broadcasted_iota(jnp.int32, (SUBLANES, c), 0)

    def body(g, h_prev):
        r0 = pl.multiple_of(g * SUBLANES, SUBLANES)
        a = a_ref[pl.ds(r0, SUBLANES), :]
        u = u_ref[pl.ds(r0, SUBLANES), :]
        for k in (1, 2, 4):
            u = a * _scan_shift(u, k, row, 0.0) + u
            a = a * _scan_shift(a, k, row, 1.0)
        h = u + a * h_prev
        yv = y_ref[0, pl.ds(r0, SUBLANES), :]
        o_ref[0, pl.ds(r0, SUBLANES), :] = h * jax.nn.gelu(yv)
        return jnp.broadcast_to(h[SUBLANES - 1:SUBLANES, :], (SUBLANES, c))

    h_ref[...] = lax.fori_loop(0, ts // SUBLANES, body, h_ref[...])


def lru_mixer(z, conv_w, conv_b, gate_w, gate_b, lam, *, c, ts=256):
    b, s, _ = z.shape
    ts = min(ts, s)
    nb = c // HEAD_DIM
    assert s % ts == 0 and ts % SUBLANES == 0
    return pl.pallas_call(
        functools.partial(_lru_kernel, ts=ts, c=c),
        grid=(b, s // ts),
        in_specs=[
            pl.BlockSpec((1, ts, c), lambda bi, si: (bi, si, 0)),
            pl.BlockSpec((1, ts, c), lambda bi, si: (bi, si, 1)),
            pl.BlockSpec((CONV_WIDTH, c), lambda bi, si: (0, 0)),
            pl.BlockSpec((1, c), lambda bi, si: (0, 0)),
            pl.BlockSpec((nb, HEAD_DIM, 2 * HEAD_DIM), lambda bi, si: (0, 0, 0)),
            pl.BlockSpec((nb, 1, 2 * HEAD_DIM), lambda bi, si: (0, 0, 0)),
            pl.BlockSpec((1, c), lambda bi, si: (0, 0)),
        ],
        out_specs=pl.BlockSpec((1, ts, c), lambda bi, si: (bi, si, 0)),
        out_shape=jax.ShapeDtypeStruct((b, s, c), F32),
        scratch_shapes=[
            pltpu.VMEM((ts + SUBLANES, c), F32),
            pltpu.VMEM((ts, c), F32),
            pltpu.VMEM((ts, c), F32),
            pltpu.VMEM((SUBLANES, c), F32),
        ],
        compiler_params=_params(("parallel", "arbitrary")),
        name="lru_mixer",
    )(z, z, conv_w, conv_b.reshape(1, c), gate_w.astype(BF16), gate_b.reshape(nb, 1, 2 * HEAD_DIM),
      lam.reshape(1, c))


def _mem_attn_kernel(q_ref, kv_ref, o_ref, *, nh):
    q = q_ref[0]
    for h in range(nh):
        sl = slice(h * HEAD_DIM, (h + 1) * HEAD_DIM)
        qh = q[:, sl].astype(BF16)
        kh = kv_ref[0, :, sl]
        vh = kv_ref[0, :, nh * HEAD_DIM + h * HEAD_DIM:nh * HEAD_DIM + (h + 1) * HEAD_DIM]
        s = lax.dot_general(qh, kh, (((1,), (1,)), ((), ())), preferred_element_type=F32) * (HEAD_DIM ** -0.5)
        m = jnp.max(s, axis=-1, keepdims=True)
        p = jnp.exp(s - m)
        p = p / jnp.sum(p, axis=-1, keepdims=True)
        o_ref[0, :, sl] = jnp.dot(p.astype(BF16), vh, preferred_element_type=F32).astype(o_ref.dtype)


def mem_attention(z, q_col_block, memkv, *, nh, tq=512):
    b, s, _ = z.shape
    nm = memkv.shape[1]
    w = nh * HEAD_DIM
    tq = min(tq, s)
    return pl.pallas_call(
        functools.partial(_mem_attn_kernel, nh=nh),
        grid=(b, s // tq),
        in_specs=[
            pl.BlockSpec((1, tq, w), lambda bi, qi: (bi, qi, q_col_block)),
            pl.BlockSpec((1, nm, 2 * w), lambda bi, qi: (bi, 0, 0)),
        ],
        out_specs=pl.BlockSpec((1, tq, w), lambda bi, qi: (bi, qi, 0)),
        out_shape=jax.ShapeDtypeStruct((b, s, w), BF16),
        compiler_params=_params(("parallel", "parallel")),
        name="mem_attention",
    )(z, memkv)


def _out_proj_kernel(main_ref, mo_ref, w1_ref, w2_ref, x_ref, o_ref):
    acc = jnp.dot(main_ref[...].astype(BF16), w1_ref[...], preferred_element_type=F32)
    acc = acc + jnp.dot(mo_ref[...].astype(BF16), w2_ref[...], preferred_element_type=F32)
    o_ref[...] = x_ref[...] + acc


def out_proj_residual(main, mo, w_out, x, *, tm=1024, tn=512):
    m, d = x.shape
    k1, k2 = main.shape[1], mo.shape[1]
    tm = min(tm, m)
    w = w_out.astype(BF16)
    return pl.pallas_call(
        _out_proj_kernel,
        grid=(m // tm, d // tn),
        in_specs=[
            pl.BlockSpec((tm, k1), lambda i, j: (i, 0)),
            pl.BlockSpec((tm, k2), lambda i, j: (i, 0)),
            pl.BlockSpec((k1, tn), lambda i, j: (0, j)),
            pl.BlockSpec((k2, tn), lambda i, j: (0, j)),
            pl.BlockSpec((tm, tn), lambda i, j: (i, j)),
        ],
        out_specs=pl.BlockSpec((tm, tn), lambda i, j: (i, j)),
        out_shape=jax.ShapeDtypeStruct((m, d), F32),
        compiler_params=_params(("parallel", "parallel")),
        name="out_proj_residual",
    )(main, mo, w[:k1], w[k1:], x)


def _forget_cumsum_kernel(zf_ref, bf_ref, c_ref, carry_ref, *, ts):
    @pl.when(pl.program_id(1) == 0)
    def _():
        carry_ref[...] = jnp.zeros((SUBLANES, LANES), F32)

    row = lax.broadcasted_iota(jnp.int32, (SUBLANES, LANES), 0)

    def body(g, carry):
        r0 = pl.multiple_of(g * SUBLANES, SUBLANES)
        v = zf_ref[0, pl.ds(r0, SUBLANES), :] + bf_ref[...]
        lf = jnp.minimum(v, 0.0) - jnp.log1p(jnp.exp(-jnp.abs(v)))
        for k in (1, 2, 4):
            lf = lf + _scan_shift(lf, k, row, 0.0)
        cg = lf + carry
        c_ref[0, pl.ds(r0, SUBLANES), :] = cg
        return jnp.broadcast_to(cg[SUBLANES - 1:SUBLANES, :], (SUBLANES, LANES))

    carry_ref[...] = lax.fori_loop(0, ts // SUBLANES, body, carry_ref[...])


def forget_cumsum(zf, b_f, *, ts=512):
    b, s, w = zf.shape
    ts = min(ts, s)
    return pl.pallas_call(
        functools.partial(_forget_cumsum_kernel, ts=ts),
        grid=(b, s // ts),
        in_specs=[
            pl.BlockSpec((1, ts, w), lambda bi, si: (bi, si, 0)),
            pl.BlockSpec((1, w), lambda bi, si: (0, 0)),
        ],
        out_specs=pl.BlockSpec((1, ts, w), lambda bi, si: (bi, si, 0)),
        out_shape=jax.ShapeDtypeStruct((b, s, w), F32),
        scratch_shapes=[pltpu.VMEM((SUBLANES, LANES), F32)],
        compiler_params=_params(("parallel", "arbitrary")),
        name="forget_cumsum",
    )(zf, b_f.reshape(1, w))


def _fox_kernel(q_ref, k_ref, v_ref, cq_ref, ck_ref, o_ref, m_ref, l_ref, acc_ref, *, tq, tk):
    qi = pl.program_id(2)
    ki = pl.program_id(3)

    @pl.when(ki == 0)
    def _():
        m_ref[...] = jnp.full((tq, 1), -jnp.inf, F32)
        l_ref[...] = jnp.zeros((tq, 1), F32)
        acc_ref[...] = jnp.zeros((tq, HEAD_DIM), F32)

    @pl.when(ki * tk <= qi * tq + (tq - 1))
    def _():
        s = lax.dot_general(q_ref[0], k_ref[0], (((1,), (1,)), ((), ())), preferred_element_type=F32)
        s = s * (HEAD_DIM ** -0.5) + cq_ref[0, 0] - ck_ref[0, 0]
        q_pos = qi * tq + lax.broadcasted_iota(jnp.int32, (tq, tk), 0)
        k_pos = ki * tk + lax.broadcasted_iota(jnp.int32, (tq, tk), 1)
        s = jnp.where(k_pos <= q_pos, s, -jnp.inf)
        m_prev = m_ref[...]
        m_new = jnp.maximum(m_prev, jnp.max(s, axis=-1, keepdims=True))
        alpha = jnp.exp(m_prev - m_new)
        p = jnp.exp(s - m_new)
        l_ref[...] = alpha * l_ref[...] + jnp.sum(p, axis=-1, keepdims=True)
        acc_ref[...] = alpha * acc_ref[...] + jnp.dot(p.astype(BF16), v_ref[0], preferred_element_type=F32)
        m_ref[...] = m_new

    @pl.when(ki == pl.num_programs(3) - 1)
    def _():
        o_ref[0] = (acc_ref[...] / l_ref[...]).astype(o_ref.dtype)


def forgetting_attention(zq, kv, c, *, nh, tq=512, tk=512):
    b, s, _ = zq.shape
    tq = min(tq, s)
    tk = min(tk, s)
    ch = jnp.transpose(c[:, :, :nh], (0, 2, 1))
    cq = ch[:, :, :, None]
    ck = ch[:, :, None, :]

    def k_block(qi, ki):
        return jnp.minimum(ki, (qi * tq + (tq - 1)) // tk)

    return pl.pallas_call(
        functools.partial(_fox_kernel, tq=tq, tk=tk),
        grid=(b, nh, s // tq, s // tk),
        in_specs=[
            pl.BlockSpec((1, tq, HEAD_DIM), lambda bi, h, qi, ki: (bi, qi, h)),
            pl.BlockSpec((1, tk, HEAD_DIM), lambda bi, h, qi, ki: (bi, k_block(qi, ki), h)),
            pl.BlockSpec((1, tk, HEAD_DIM), lambda bi, h, qi, ki: (bi, k_block(qi, ki), nh + h)),
            pl.BlockSpec((1, 1, tq, 1), lambda bi, h, qi, ki: (bi, h, qi, 0)),
            pl.BlockSpec((1, 1, 1, tk), lambda bi, h, qi, ki: (bi, h, 0, k_block(qi, ki))),
        ],
        out_specs=pl.BlockSpec((1, tq, HEAD_DIM), lambda bi, h, qi, ki: (bi, qi, h)),
        out_shape=jax.ShapeDtypeStruct((b, s, nh * HEAD_DIM), BF16),
        scratch_shapes=[
            pltpu.VMEM((tq, 1), F32),
            pltpu.VMEM((tq, 1), F32),
            pltpu.VMEM((tq, HEAD_DIM), F32),
        ],
        compiler_params=_params(("parallel", "parallel", "parallel", "arbitrary")),
        name="forgetting_attention",
    )(zq, kv, kv, cq, ck)


def _topk_rows(s, iota, k, payload=None):
    big = float(s.shape[0])
    vals, picks = [], []
    for _ in range(k):
        m = jnp.max(s, axis=0, keepdims=True)
        pos = jnp.min(jnp.where(s == m, iota, big), axis=0, keepdims=True)
        sel = iota == pos
        if payload is None:
            picks.append(pos)
        else:
            picks.append(jnp.max(jnp.where(sel, payload, -1.0), axis=0, keepdims=True))
        vals.append(m)
        s = jnp.where(sel, -jnp.inf, s)
    return vals, picks


def _route_kernel(q_ref, sk_ref, idx_ref, gt_ref, *, tm):
    iota_k = lax.broadcasted_iota(jnp.int32, (N_KEYS, tm), 0).astype(F32)
    iota_c = lax.broadcasted_iota(jnp.int32, (PEER_TOPK * PEER_TOPK, tm), 0).astype(F32)
    idx_rows, gate_rows = [], []
    for h in range(PEER_HEADS):
        halves = []
        for p in range(2):
            col = (2 * h + p) * HEAD_DIM
            st = lax.dot_general(sk_ref[h, p], q_ref[:, col:col + HEAD_DIM], (((1,), (1,)), ((), ())),
                                 preferred_element_type=F32)
            halves.append(_topk_rows(st, iota_k, PEER_TOPK))
        (v1, i1), (v2, i2) = halves
        s2 = jnp.concatenate(v2, axis=0)
        e2 = jnp.concatenate(i2, axis=0)
        cand = jnp.concatenate([v1[a] + s2 for a in range(PEER_TOPK)], axis=0)
        cand_e = jnp.concatenate([i1[a] * float(N_KEYS) + e2 for a in range(PEER_TOPK)], axis=0)
        best, experts = _topk_rows(cand, iota_c, PEER_TOPK, payload=cand_e)
        bs = jnp.concatenate(best, axis=0)
        ex = jnp.exp(bs - bs[0:1, :])
        gate_rows.append(ex / jnp.sum(ex, axis=0, keepdims=True))
        idx_rows.append(jnp.concatenate(experts, axis=0))
    gt_ref[...] = jnp.concatenate(gate_rows, axis=0)
    idx_ref[...] = jnp.transpose(jnp.concatenate(idx_rows, axis=0)).astype(jnp.int32)


def peer_route(q, subkeys, *, tm=256):
    t, dq = q.shape
    tm = min(tm, t)
    nsel = PEER_HEADS * PEER_TOPK
    return pl.pallas_call(
        functools.partial(_route_kernel, tm=tm),
        grid=(t // tm,),
        in_specs=[
            pl.BlockSpec((tm, dq), lambda i: (i, 0)),
            pl.BlockSpec((PEER_HEADS, 2, N_KEYS, HEAD_DIM), lambda i: (0, 0, 0, 0)),
        ],
        out_specs=[
            pl.BlockSpec((tm, nsel), lambda i: (i, 0)),
            pl.BlockSpec((nsel, tm), lambda i: (0, i)),
        ],
        out_shape=[jax.ShapeDtypeStruct((t, nsel), jnp.int32), jax.ShapeDtypeStruct((nsel, t), F32)],
        compiler_params=_params(("parallel",)),
        name="peer_route",
    )(q, subkeys.astype(BF16))


def _expert_kernel(idx_ref, gt_ref, x_ref, g_ref, tab_ref, o_ref, hn_ref, buf, sem, *, tb, nsel, nslot, d):
    def issue(t, slot):
        for j in range(nsel):
            pltpu.make_async_copy(tab_ref.at[pl.ds(idx_ref[t, j], 1), :],
                                  buf.at[slot, pl.ds(j, 1), :], sem.at[slot]).start()

    def wait(slot):
        pltpu.make_async_copy(tab_ref.at[pl.ds(0, nsel), :], buf.at[slot], sem.at[slot]).wait()

    for t0 in range(nslot - 1):
        issue(t0, t0)

    hn_ref[...] = _head_rmsnorm(x_ref[...], g_ref[...])
    lane = lax.broadcasted_iota(jnp.int32, (nsel, tb), 1)

    def body(g, carry):
        base = pl.multiple_of(g * SUBLANES, SUBLANES)
        hn8 = hn_ref[pl.ds(base, SUBLANES), :]
        rows = []
        for r in range(SUBLANES):
            t = base + r
            nxt = t + (nslot - 1)
            nxt_slot = (r + nslot - 1) % nslot
            if r + nslot - 1 < SUBLANES:
                issue(nxt, nxt_slot)
            else:
                pl.when(nxt < tb)(functools.partial(issue, nxt, nxt_slot))
            slot = r % nslot
            wait(slot)
            gate = jnp.sum(jnp.where(lane == t, gt_ref[...], 0.0), axis=-1, keepdims=True)
            acc = buf[slot, :, 0:LANES] * hn8[r:r + 1, 0:LANES]
            for k in range(1, d // LANES):
                acc = acc + buf[slot, :, k * LANES:(k + 1) * LANES] * hn8[r:r + 1, k * LANES:(k + 1) * LANES]
            act = jnp.sum(acc, axis=-1, keepdims=True)
            w = gate * jax.nn.gelu(act)
            rows.append(jnp.concatenate(
                [jnp.sum(buf[slot, :, d + k * LANES:d + (k + 1) * LANES] * w, axis=0, keepdims=True)
                 for k in range(d // LANES)], axis=1))
        o_ref[pl.ds(base, SUBLANES), :] = x_ref[pl.ds(base, SUBLANES), :] + jnp.concatenate(rows, axis=0)
        return carry

    lax.fori_loop(0, tb // SUBLANES, body, 0)


def peer_experts(idx, gates_t, x, norm_g, table, *, tb=128, nslot=4):
    t, d = x.shape
    nsel = idx.shape[1]
    tb = min(tb, t)
    assert t % tb == 0 and tb % SUBLANES == 0 and SUBLANES % nslot == 0
    return pl.pallas_call(
        functools.partial(_expert_kernel, tb=tb, nsel=nsel, nslot=nslot, d=d),
        grid=(t // tb,),
        in_specs=[
            pl.BlockSpec((tb, nsel), lambda i: (i, 0), memory_space=pltpu.SMEM),
            pl.BlockSpec((nsel, tb), lambda i: (0, i)),
            pl.BlockSpec((tb, d), lambda i: (i, 0)),
            pl.BlockSpec((1, d), lambda i: (0, 0)),
            pl.BlockSpec(memory_space=pl.ANY),
        ],
        out_specs=pl.BlockSpec((tb, d), lambda i: (i, 0)),
        out_shape=jax.ShapeDtypeStruct((t, d), F32),
        scratch_shapes=[
            pltpu.VMEM((tb, d), F32),
            pltpu.VMEM((nslot, nsel, 2 * d), F32),
            pltpu.SemaphoreType.DMA((nslot,)),
        ],
        compiler_params=_params(("arbitrary",)),
        name="peer_experts",
    )(idx, gates_t, x, norm_g.reshape(1, d), table)


def _tile_gain(g, n):
    return jnp.tile(g, n)


def kernel(x, mem, a_norm_g, a_w_in, a_conv_w, a_conv_b, a_gate_w, a_gate_b, a_lambda, a_w_out, s_norm_g, s_w_kvf, s_b_f, s_k_norm_g, b_norm_g, b_w_in, b_q_norm_g, b_w_out, m_norm_g, m_w_kv, m_q_norm_g, m_k_norm_g, p_norm_g, p_w_q, p_subkeys, p_u, p_v):
    b, s, d = x.shape
    t = b * s
    n_mem = mem.shape[1]
    depth = m_norm_g.shape[0]
    n_a = a_norm_g.shape[0]
    lru_w = a_conv_w.shape[-1]
    mem_w = m_w_kv.shape[-1] // 2
    fox_w = b_w_in.shape[-1] - mem_w
    n_mem_heads = mem_w // HEAD_DIM
    n_fox_heads = fox_w // HEAD_DIM
    n_fgate = s_w_kvf.shape[1] - 2 * fox_w

    xf = x.reshape(t, d)
    memf = mem.reshape(b * n_mem, d)
    kv_shared = c_shared = None
    for layer in range(depth):
        mem_gain = jnp.concatenate([_tile_gain(m_k_norm_g[layer], n_mem_heads), jnp.ones((mem_w,), F32)])
        memkv = norm_matmul(memf, m_norm_g[layer], m_w_kv[layer], head_gain=mem_gain, head_tiles=(0, 1),
                            out_dtype=BF16, tn=mem_w).reshape(b, n_mem, 2 * mem_w)
        mq_gain = _tile_gain(m_q_norm_g[layer], n_mem_heads)
        if layer < n_a:
            i = layer
            n_in = 2 * lru_w + mem_w
            gain = jnp.concatenate([jnp.ones((2 * lru_w,), F32), mq_gain])
            z = norm_matmul(xf, a_norm_g[i], a_w_in[i], head_gain=gain,
                            head_tiles=(2 * lru_w // mem_w, n_in // mem_w), tn=mem_w).reshape(b, s, n_in)
            main = lru_mixer(z, a_conv_w[i], a_conv_b[i], a_gate_w[i], a_gate_b[i], a_lambda[i], c=lru_w)
            q_col_block = 2 * lru_w // mem_w
            w_out = a_w_out[i]
        else:
            if layer == n_a:
                k_gain = jnp.concatenate([_tile_gain(s_k_norm_g, n_fox_heads), jnp.ones((fox_w,), F32)])
                kv_shared = norm_matmul(xf, s_norm_g, s_w_kvf[:, :2 * fox_w], head_gain=k_gain,
                                        head_tiles=(0, fox_w // mem_w), out_dtype=BF16,
                                        tn=mem_w).reshape(b, s, 2 * fox_w)
                w_f = jnp.pad(s_w_kvf[:, 2 * fox_w:], ((0, 0), (0, LANES - n_fgate)))
                zf = norm_matmul(xf, s_norm_g, w_f, tn=LANES).reshape(b, s, LANES)
                c_shared = forget_cumsum(zf, jnp.pad(s_b_f, (0, LANES - n_fgate)))
            j = layer - n_a
            gain = jnp.concatenate([_tile_gain(b_q_norm_g[j], n_fox_heads), mq_gain])
            z = norm_matmul(xf, b_norm_g[j], b_w_in[j], head_gain=gain, head_tiles=(0, (fox_w + mem_w) // mem_w),
                            out_dtype=BF16, tn=mem_w).reshape(b, s, fox_w + mem_w)
            main = forgetting_attention(z, kv_shared, c_shared, nh=n_fox_heads)
            q_col_block = fox_w // mem_w
            w_out = b_w_out[j]
        mo = mem_attention(z, q_col_block, memkv, nh=n_mem_heads)
        xf = out_proj_residual(main.reshape(t, -1), mo.reshape(t, mem_w), w_out, xf)

        q = norm_matmul(xf, p_norm_g[layer], p_w_q[layer], out_dtype=BF16)
        idx, gates_t = peer_route(q, p_subkeys[layer])
        table = jnp.concatenate([p_u[layer], p_v[layer]], axis=1)
        xf = peer_experts(idx, gates_t, xf, p_norm_g[layer], table)
    return xf.reshape(b, s, d)
```

```python
import functools

import jax
import jax.numpy as jnp
from jax import lax
from jax.experimental import pallas as pl
from jax.experimental.pallas import tpu as pltpu

F32 = jnp.float32
BF16 = jnp.bfloat16

HEAD_DIM = 128
LANES = 128
SUBLANES = 8
CONV_WIDTH = 4
LRU_C = 8.0
N_KEYS = 128
PEER_HEADS = 8
PEER_TOPK = 16
RMS_EPS = 1e-6
VMEM_LIMIT_BYTES = 48 * 1024 * 1024


def _params(semantics):
    return pltpu.CompilerParams(dimension_semantics=semantics, vmem_limit_bytes=VMEM_LIMIT_BYTES)


def _head_rmsnorm(y, gain):
    ms = jnp.mean(y * y, axis=-1, keepdims=True)
    return (y * lax.rsqrt(ms + RMS_EPS)) * gain


def _norm_matmul_kernel(x_ref, g_ref, w_ref, hg_ref, o_ref, xn_ref, *, head_lo, head_hi):
    j = pl.program_id(1)

    @pl.when(j == 0)
    def _():
        x = x_ref[...]
        ms = jnp.mean(x * x, axis=-1, keepdims=True)
        xn_ref[...] = ((x * lax.rsqrt(ms + RMS_EPS)) * g_ref[...]).astype(BF16)

    y = jnp.dot(xn_ref[...], w_ref[...], preferred_element_type=F32)
    tn = y.shape[1]

    def store_normed():
        for c in range(tn // HEAD_DIM):
            sl = slice(c * HEAD_DIM, (c + 1) * HEAD_DIM)
            o_ref[:, sl] = _head_rmsnorm(y[:, sl], hg_ref[:, sl]).astype(o_ref.dtype)

    def store_plain():
        o_ref[...] = y.astype(o_ref.dtype)

    if head_hi <= head_lo:
        store_plain()
    else:
        normed = jnp.logical_and(j >= head_lo, j < head_hi)
        pl.when(normed)(store_normed)
        pl.when(jnp.logical_not(normed))(store_plain)


def norm_matmul(x, g, w, *, head_gain=None, head_tiles=(0, 0), out_dtype=F32, tm=1024, tn=512):
    m, d = x.shape
    n = w.shape[1]
    tm = min(tm, m)
    tn = min(tn, n)
    assert m % tm == 0 and n % tn == 0 and tn % HEAD_DIM == 0
    if head_gain is None:
        head_gain = jnp.ones((n,), F32)
    return pl.pallas_call(
        functools.partial(_norm_matmul_kernel, head_lo=head_tiles[0], head_hi=head_tiles[1]),
        grid=(m // tm, n // tn),
        in_specs=[
            pl.BlockSpec((tm, d), lambda i, j: (i, 0)),
            pl.BlockSpec((1, d), lambda i, j: (0, 0)),
            pl.BlockSpec((d, tn), lambda i, j: (0, j)),
            pl.BlockSpec((1, tn), lambda i, j: (0, j)),
        ],
        out_specs=pl.BlockSpec((tm, tn), lambda i, j: (i, j)),
        out_shape=jax.ShapeDtypeStruct((m, n), out_dtype),
        scratch_shapes=[pltpu.VMEM((tm, d), BF16)],
        compiler_params=_params(("parallel", "arbitrary")),
        name="norm_matmul",
    )(x, g.reshape(1, d), w.astype(BF16), head_gain.reshape(1, n))


def _scan_shift(v, k, row, fill):
    return jnp.where(row >= k, pltpu.roll(v, k, 0), fill)


def _lru_kernel(x_ref, y_ref, cw_ref, cb_ref, gw_ref, gb_ref, lam_ref, o_ref,
                xe_ref, a_ref, u_ref, h_ref, *, ts, c):
    s = pl.program_id(1)

    @pl.when(s == 0)
    def _():
        xe_ref[0:SUBLANES, :] = jnp.zeros((SUBLANES, c), F32)
        h_ref[...] = jnp.zeros((SUBLANES, c), F32)

    x = x_ref[0]
    xe_ref[SUBLANES:SUBLANES + ts, :] = x
    cw = cw_ref[...]
    xc = cw[3:4, :] * x
    for k in range(CONV_WIDTH - 1):
        off = SUBLANES - (CONV_WIDTH - 1) + k
        xc = xc + cw[k:k + 1, :] * xe_ref[off:off + ts, :]
    xc = xc + cb_ref[...]
    xe_ref[0:SUBLANES, :] = x[ts - SUBLANES:ts, :]

    lam = lam_ref[...]
    softplus_neg_lam = jnp.maximum(-lam, 0.0) + jnp.log1p(jnp.exp(-jnp.abs(lam)))
    xcb = xc.astype(BF16)
    for n in range(c // HEAD_DIM):
        sl = slice(n * HEAD_DIM, (n + 1) * HEAD_DIM)
        gates = jnp.dot(xcb[:, sl], gw_ref[n], preferred_element_type=F32) + gb_ref[n]
        r = jax.nn.sigmoid(gates[:, :HEAD_DIM])
        i = jax.nn.sigmoid(gates[:, HEAD_DIM:])
        log_a = (-LRU_C * r) * softplus_neg_lam[:, sl]
        a = jnp.exp(log_a)
        a_ref[:, sl] = a
        u_ref[:, sl] = jnp.sqrt(-jnp.tanh(log_a) * (a * a + 1.0)) * (i * xc[:, sl])

    row = lax.broadcasted_iota(jnp.int32, (SUBLANES, c), 0)

    def body(g, h_prev):
        r0 = pl.multiple_of(g * SUBLANES, SUBLANES)
        a = a_ref[pl.ds(r0, SUBLANES), :]
        u = u_ref[pl.ds(r0, SUBLANES), :]
        for k in (1, 2, 4):
            u = a * _scan_shift(u, k, row, 0.0) + u
            a = a * _scan_shift(a, k, row, 1.0)
        h = u + a * h_prev
        yv = y_ref[0, pl.ds(r0, SUBLANES), :]
        o_ref[0, pl.ds(r0, SUBLANES), :] = h * jax.nn.gelu(yv)
        return jnp.broadcast_to(h[SUBLANES - 1:SUBLANES, :], (SUBLANES, c))

    h_ref[...] = lax.fori_loop(0, ts // SUBLANES, body, h_ref[...])


def lru_mixer(z, conv_w, conv_b, gate_w, gate_b, lam, *, c, ts=256):
    b, s, _ = z.shape
    ts = min(ts, s)
    nb = c // HEAD_DIM
    assert s % ts == 0 and ts % SUBLANES == 0
    return pl.pallas_call(
        functools.partial(_lru_kernel, ts=ts, c=c),
        grid=(b, s // ts),
        in_specs=[
            pl.BlockSpec((1, ts, c), lambda bi, si: (bi, si, 0)),
            pl.BlockSpec((1, ts, c), lambda bi, si: (bi, si, 1)),
            pl.BlockSpec((CONV_WIDTH, c), lambda bi, si: (0, 0)),
            pl.BlockSpec((1, c), lambda bi, si: (0, 0)),
            pl.BlockSpec((nb, HEAD_DIM, 2 * HEAD_DIM), lambda bi, si: (0, 0, 0)),
            pl.BlockSpec((nb, 1, 2 * HEAD_DIM), lambda bi, si: (0, 0, 0)),
            pl.BlockSpec((1, c), lambda bi, si: (0, 0)),
        ],
        out_specs=pl.BlockSpec((1, ts, c), lambda bi, si: (bi, si, 0)),
        out_shape=jax.ShapeDtypeStruct((b, s, c), F32),
        scratch_shapes=[
            pltpu.VMEM((ts + SUBLANES, c), F32),
            pltpu.VMEM((ts, c), F32),
            pltpu.VMEM((ts, c), F32),
            pltpu.VMEM((SUBLANES, c), F32),
        ],
        compiler_params=_params(("parallel", "arbitrary")),
        name="lru_mixer",
    )(z, z, conv_w, conv_b.reshape(1, c), gate_w.astype(BF16), gate_b.reshape(nb, 1, 2 * HEAD_DIM),
      lam.reshape(1, c))


def _mem_attn_kernel(q_ref, kv_ref, o_ref, *, nh):
    q = q_ref[0]
    for h in range(nh):
        sl = slice(h * HEAD_DIM, (h + 1) * HEAD_DIM)
        qh = q[:, sl].astype(BF16)
        kh = kv_ref[0, :, sl]
        vh = kv_ref[0, :, nh * HEAD_DIM + h * HEAD_DIM:nh * HEAD_DIM + (h + 1) * HEAD_DIM]
        s = lax.dot_general(qh, kh, (((1,), (1,)), ((), ())), preferred_element_type=F32) * (HEAD_DIM ** -0.5)
        m = jnp.max(s, axis=-1, keepdims=True)
        p = jnp.exp(s - m)
        p = p / jnp.sum(p, axis=-1, keepdims=True)
        o_ref[0, :, sl] = jnp.dot(p.astype(BF16), vh, preferred_element_type=F32).astype(o_ref.dtype)


def mem_attention(z, q_col_block, memkv, *, nh, tq=512):
    b, s, _ = z.shape
    nm = memkv.shape[1]
    w = nh * HEAD_DIM
    tq = min(tq, s)
    return pl.pallas_call(
        functools.partial(_mem_attn_kernel, nh=nh),
        grid=(b, s // tq),
        in_specs=[
            pl.BlockSpec((1, tq, w), lambda bi, qi: (bi, qi, q_col_block)),
            pl.BlockSpec((1, nm, 2 * w), lambda bi, qi: (bi, 0, 0)),
        ],
        out_specs=pl.BlockSpec((1, tq, w), lambda bi, qi: (bi, qi, 0)),
        out_shape=jax.ShapeDtypeStruct((b, s, w), BF16),
        compiler_params=_params(("parallel", "parallel")),
        name="mem_attention",
    )(z, memkv)


def _out_proj_kernel(main_ref, mo_ref, w1_ref, w2_ref, x_ref, o_ref):
    acc = jnp.dot(main_ref[...].astype(BF16), w1_ref[...], preferred_element_type=F32)
    acc = acc + jnp.dot(mo_ref[...].astype(BF16), w2_ref[...], preferred_element_type=F32)
    o_ref[...] = x_ref[...] + acc


def out_proj_residual(main, mo, w_out, x, *, tm=1024, tn=512):
    m, d = x.shape
    k1, k2 = main.shape[1], mo.shape[1]
    tm = min(tm, m)
    w = w_out.astype(BF16)
    return pl.pallas_call(
        _out_proj_kernel,
        grid=(m // tm, d // tn),
        in_specs=[
            pl.BlockSpec((tm, k1), lambda i, j: (i, 0)),
            pl.BlockSpec((tm, k2), lambda i, j: (i, 0)),
            pl.BlockSpec((k1, tn), lambda i, j: (0, j)),
            pl.BlockSpec((k2, tn), lambda i, j: (0, j)),
            pl.BlockSpec((tm, tn), lambda i, j: (i, j)),
        ],
        out_specs=pl.BlockSpec((tm, tn), lambda i, j: (i, j)),
        out_shape=jax.ShapeDtypeStruct((m, d), F32),
        compiler_params=_params(("parallel", "parallel")),
        name="out_proj_residual",
    )(main, mo, w[:k1], w[k1:], x)


def _forget_cumsum_kernel(zf_ref, bf_ref, c_ref, carry_ref, *, ts):
    @pl.when(pl.program_id(1) == 0)
    def _():
        carry_ref[...] = jnp.zeros((SUBLANES, LANES), F32)

    row = lax.broadcasted_iota(jnp.int32, (SUBLANES, LANES), 0)

    def body(g, carry):
        r0 = pl.multiple_of(g * SUBLANES, SUBLANES)
        v = zf_ref[0, pl.ds(r0, SUBLANES), :] + bf_ref[...]
        lf = jnp.minimum(v, 0.0) - jnp.log1p(jnp.exp(-jnp.abs(v)))
        for k in (1, 2, 4):
            lf = lf + _scan_shift(lf, k, row, 0.0)
        cg = lf + carry
        c_ref[0, pl.ds(r0, SUBLANES), :] = cg
        return jnp.broadcast_to(cg[SUBLANES - 1:SUBLANES, :], (SUBLANES, LANES))

    carry_ref[...] = lax.fori_loop(0, ts // SUBLANES, body, carry_ref[...])


def forget_cumsum(zf, b_f, *, ts=512):
    b, s, w = zf.shape
    ts = min(ts, s)
    return pl.pallas_call(
        functools.partial(_forget_cumsum_kernel, ts=ts),
        grid=(b, s // ts),
        in_specs=[
            pl.BlockSpec((1, ts, w), lambda bi, si: (bi, si, 0)),
            pl.BlockSpec((1, w), lambda bi, si: (0, 0)),
        ],
        out_specs=pl.BlockSpec((1, ts, w), lambda bi, si: (bi, si, 0)),
        out_shape=jax.ShapeDtypeStruct((b, s, w), F32),
        scratch_shapes=[pltpu.VMEM((SUBLANES, LANES), F32)],
        compiler_params=_params(("parallel", "arbitrary")),
        name="forget_cumsum",
    )(zf, b_f.reshape(1, w))


def _fox_kernel(q_ref, k_ref, v_ref, cq_ref, ck_ref, o_ref, m_ref, l_ref, acc_ref, *, tq, tk):
    qi = pl.program_id(2)
    log2e = 1.4426950408889634
    q = q_ref[0]
    cq2 = cq_ref[0, 0] * log2e
    m_ref[...] = jnp.full((tq, 1), -jnp.inf, F32)
    l_ref[...] = jnp.zeros((tq, 1), F32)
    acc_ref[...] = jnp.zeros((tq, HEAD_DIM), F32)

    def logits2(kb):
        k0 = pl.multiple_of(kb * tk, tk)
        s = lax.dot_general(q, k_ref[0, pl.ds(k0, tk), :], (((1,), (1,)), ((), ())), preferred_element_type=F32)
        return s * (HEAD_DIM ** -0.5 * log2e) + cq2 - ck_ref[0, 0, kb] * log2e, k0

    def update(s, k0):
        m_prev = m_ref[...]
        m_new = jnp.maximum(m_prev, jnp.max(s, axis=-1, keepdims=True))
        alpha = jnp.exp2(m_prev - m_new)
        p = jnp.exp2(s - m_new)
        l_ref[...] = alpha * l_ref[...] + jnp.sum(p, axis=-1, keepdims=True)
        acc_ref[...] = alpha * acc_ref[...] + jnp.dot(p.astype(BF16), v_ref[0, pl.ds(k0, tk), :],
                                                      preferred_element_type=F32)
        m_ref[...] = m_new

    n_diag = tq // tk
    n_below = qi * n_diag

    def body(kb, carry):
        s, k0 = logits2(kb)
        update(s, k0)
        return carry

    lax.fori_loop(0, n_below, body, 0)
    q_pos = lax.broadcasted_iota(jnp.int32, (tq, tk), 0)
    k_pos = lax.broadcasted_iota(jnp.int32, (tq, tk), 1)
    for j in range(n_diag):
        s, k0 = logits2(n_below + j)
        update(jnp.where(k_pos + j * tk <= q_pos, s, -jnp.inf), k0)
    o_ref[0] = (acc_ref[...] / l_ref[...]).astype(o_ref.dtype)


def forgetting_attention(zq, kv, c, *, nh, tq=512, tk=512):
    b, s, _ = zq.shape
    tq = min(tq, s)
    tk = min(tk, tq)
    assert s % tq == 0 and tq % tk == 0
    ch = jnp.transpose(c[:, :, :nh], (0, 2, 1))
    cq = ch[:, :, :, None]
    ck = ch.reshape(b, nh, s // tk, 1, tk)
    return pl.pallas_call(
        functools.partial(_fox_kernel, tq=tq, tk=tk),
        grid=(b, nh, s // tq),
        in_specs=[
            pl.BlockSpec((1, tq, HEAD_DIM), lambda bi, h, qi: (bi, qi, h)),
            pl.BlockSpec((1, s, HEAD_DIM), lambda bi, h, qi: (bi, 0, h)),
            pl.BlockSpec((1, s, HEAD_DIM), lambda bi, h, qi: (bi, 0, nh + h)),
            pl.BlockSpec((1, 1, tq, 1), lambda bi, h, qi: (bi, h, qi, 0)),
            pl.BlockSpec((1, 1, s // tk, 1, tk), lambda bi, h, qi: (bi, h, 0, 0, 0)),
        ],
        out_specs=pl.BlockSpec((1, tq, HEAD_DIM), lambda bi, h, qi: (bi, qi, h)),
        out_shape=jax.ShapeDtypeStruct((b, s, nh * HEAD_DIM), BF16),
        scratch_shapes=[
            pltpu.VMEM((tq, 1), F32),
            pltpu.VMEM((tq, 1), F32),
            pltpu.VMEM((tq, HEAD_DIM), F32),
        ],
        compiler_params=_params(("parallel", "parallel", "arbitrary")),
        name="forgetting_attention",
    )(zq, kv, kv, cq, ck)


def _topk_rows(s, iota, k, payload=None):
    big = float(s.shape[0])
    vals, picks = [], []
    for _ in range(k):
        m = jnp.max(s, axis=0, keepdims=True)
        pos = jnp.min(jnp.where(s == m, iota, big), axis=0, keepdims=True)
        sel = iota == pos
        if payload is None:
            picks.append(pos)
        else:
            picks.append(jnp.max(jnp.where(sel, payload, -1.0), axis=0, keepdims=True))
        vals.append(m)
        s = jnp.where(sel, -jnp.inf, s)
    return vals, picks


def _route_kernel(q_ref, sk_ref, idx_ref, gt_ref, *, tm):
    iota_k = lax.broadcasted_iota(jnp.int32, (N_KEYS, tm), 0).astype(F32)
    half_k = PEER_TOPK // 2
    n_cand = PEER_TOPK + (half_k - 1) * half_k + half_k
    iota_c = lax.broadcasted_iota(jnp.int32, (n_cand, tm), 0).astype(F32)
    idx_rows, gate_rows = [], []
    for h in range(PEER_HEADS):
        halves = []
        for p in range(2):
            col = (2 * h + p) * HEAD_DIM
            st = lax.dot_general(sk_ref[h, p], q_ref[:, col:col + HEAD_DIM], (((1,), (1,)), ((), ())),
                                 preferred_element_type=F32)
            halves.append(_topk_rows(st, iota_k, PEER_TOPK))
        (v1, i1), (v2, i2) = halves
        s2 = jnp.concatenate(v2, axis=0)
        e2 = jnp.concatenate(i2, axis=0)
        s1_tail = jnp.concatenate(v1[half_k:], axis=0)
        e1_tail = jnp.concatenate(i1[half_k:], axis=0)
        cand = jnp.concatenate([v1[0] + s2] + [v1[a] + s2[:half_k] for a in range(1, half_k)]
                               + [s1_tail + v2[0]], axis=0)
        cand_e = jnp.concatenate([i1[0] * float(N_KEYS) + e2]
                                 + [i1[a] * float(N_KEYS) + e2[:half_k] for a in range(1, half_k)]
                                 + [e1_tail * float(N_KEYS) + i2[0]], axis=0)
        best, experts = _topk_rows(cand, iota_c, PEER_TOPK, payload=cand_e)
        bs = jnp.concatenate(best, axis=0)
        ex = jnp.exp(bs - bs[0:1, :])
        gate_rows.append(ex / jnp.sum(ex, axis=0, keepdims=True))
        idx_rows.append(jnp.concatenate(experts, axis=0))
    gt_ref[...] = jnp.concatenate(gate_rows, axis=0)
    idx_ref[...] = jnp.transpose(jnp.concatenate(idx_rows, axis=0)).astype(jnp.int32)


def peer_route(q, subkeys, *, tm=256):
    t, dq = q.shape
    tm = min(tm, t)
    nsel = PEER_HEADS * PEER_TOPK
    return pl.pallas_call(
        functools.partial(_route_kernel, tm=tm),
        grid=(t // tm,),
        in_specs=[
            pl.BlockSpec((tm, dq), lambda i: (i, 0)),
            pl.BlockSpec((PEER_HEADS, 2, N_KEYS, HEAD_DIM), lambda i: (0, 0, 0, 0)),
        ],
        out_specs=[
            pl.BlockSpec((tm, nsel), lambda i: (i, 0)),
            pl.BlockSpec((nsel, tm), lambda i: (0, i)),
        ],
        out_shape=[jax.ShapeDtypeStruct((t, nsel), jnp.int32), jax.ShapeDtypeStruct((nsel, t), F32)],
        compiler_params=_params(("parallel",)),
        name="peer_route",
    )(q, subkeys.astype(BF16))


def _pack_bf16_pairs(w):
    half = w.shape[1] // 2
    bits = lax.bitcast_convert_type(w.astype(BF16), jnp.uint16).astype(jnp.uint32)
    return bits[:, :half] | (bits[:, half:] << 16)


def pack_expert_tables(u, v):
    return jnp.concatenate([_pack_bf16_pairs(u), _pack_bf16_pairs(v)], axis=1)


def _unpack_lo(w):
    return lax.bitcast_convert_type(w << 16, F32)


def _unpack_hi(w):
    return lax.bitcast_convert_type(w & jnp.uint32(0xFFFF0000), F32)


def _expert_kernel(idx_ref, gt_ref, x_ref, g_ref, tab_ref, o_ref, hn_ref, buf, sem, *, tb, nsel, nslot, d):
    half = d // 2
    ntile = half // LANES

    def issue(t, slot):
        for j in range(nsel):
            pltpu.make_async_copy(tab_ref.at[pl.ds(idx_ref[t, j], 1), :],
                                  buf.at[slot, pl.ds(j, 1), :], sem.at[slot]).start(priority=j % 2)

    def wait(slot):
        pltpu.make_async_copy(tab_ref.at[pl.ds(0, nsel), :], buf.at[slot], sem.at[slot]).wait()

    for t0 in range(nslot - 1):
        issue(t0, t0)

    hn_ref[...] = _head_rmsnorm(x_ref[...], g_ref[...])
    lane = lax.broadcasted_iota(jnp.int32, (nsel, tb), 1)

    def body(g, carry):
        base = pl.multiple_of(g * SUBLANES, SUBLANES)
        hn8 = hn_ref[pl.ds(base, SUBLANES), :]
        rows = []
        for r in range(SUBLANES):
            t = base + r
            nxt = t + (nslot - 1)
            nxt_slot = (r + nslot - 1) % nslot
            if r + nslot - 1 < SUBLANES:
                issue(nxt, nxt_slot)
            else:
                pl.when(nxt < tb)(functools.partial(issue, nxt, nxt_slot))
            slot = r % nslot
            wait(slot)
            gate = jnp.sum(jnp.where(lane == t, gt_ref[...], 0.0), axis=-1, keepdims=True)
            acc = None
            for k in range(ntile):
                words = buf[slot, :, k * LANES:(k + 1) * LANES]
                part = (_unpack_lo(words) * hn8[r:r + 1, k * LANES:(k + 1) * LANES]
                        + _unpack_hi(words) * hn8[r:r + 1, half + k * LANES:half + (k + 1) * LANES])
                acc = part if acc is None else acc + part
            act = jnp.sum(acc, axis=-1, keepdims=True)
            w = gate * jax.nn.gelu(act)
            lo_tiles, hi_tiles = [], []
            for k in range(ntile):
                words = buf[slot, :, half + k * LANES:half + (k + 1) * LANES]
                lo_tiles.append(jnp.sum(_unpack_lo(words) * w, axis=0, keepdims=True))
                hi_tiles.append(jnp.sum(_unpack_hi(words) * w, axis=0, keepdims=True))
            rows.append(jnp.concatenate(lo_tiles + hi_tiles, axis=1))
        o_ref[pl.ds(base, SUBLANES), :] = x_ref[pl.ds(base, SUBLANES), :] + jnp.concatenate(rows, axis=0)
        return carry

    lax.fori_loop(0, tb // SUBLANES, body, 0)


def peer_experts(idx, gates_t, x, norm_g, table, *, tb=128, nslot=4):
    t, d = x.shape
    nsel = idx.shape[1]
    tb = min(tb, t)
    assert t % tb == 0 and tb % SUBLANES == 0 and SUBLANES % nslot == 0
    return pl.pallas_call(
        functools.partial(_expert_kernel, tb=tb, nsel=nsel, nslot=nslot, d=d),
        grid=(t // tb,),
        in_specs=[
            pl.BlockSpec((tb, nsel), lambda i: (i, 0), memory_space=pltpu.SMEM),
            pl.BlockSpec((nsel, tb), lambda i: (0, i)),
            pl.BlockSpec((tb, d), lambda i: (i, 0)),
            pl.BlockSpec((1, d), lambda i: (0, 0)),
            pl.BlockSpec(memory_space=pl.ANY),
        ],
        out_specs=pl.BlockSpec((tb, d), lambda i: (i, 0)),
        out_shape=jax.ShapeDtypeStruct((t, d), F32),
        scratch_shapes=[
            pltpu.VMEM((tb, d), F32),
            pltpu.VMEM((nslot, nsel, d), jnp.uint32),
            pltpu.SemaphoreType.DMA((nslot,)),
        ],
        compiler_params=_params(("arbitrary",)),
        name="peer_experts",
    )(idx, gates_t, x, norm_g.reshape(1, d), table)


def _tile_gain(g, n):
    return jnp.tile(g, n)


def kernel(x, mem, a_norm_g, a_w_in, a_conv_w, a_conv_b, a_gate_w, a_gate_b, a_lambda, a_w_out, s_norm_g, s_w_kvf, s_b_f, s_k_norm_g, b_norm_g, b_w_in, b_q_norm_g, b_w_out, m_norm_g, m_w_kv, m_q_norm_g, m_k_norm_g, p_norm_g, p_w_q, p_subkeys, p_u, p_v):
    b, s, d = x.shape
    t = b * s
    n_mem = mem.shape[1]
    depth = m_norm_g.shape[0]
    n_a = a_norm_g.shape[0]
    lru_w = a_conv_w.shape[-1]
    mem_w = m_w_kv.shape[-1] // 2
    fox_w = b_w_in.shape[-1] - mem_w
    n_mem_heads = mem_w // HEAD_DIM
    n_fox_heads = fox_w // HEAD_DIM
    n_fgate = s_w_kvf.shape[1] - 2 * fox_w

    xf = x.reshape(t, d)
    memf = mem.reshape(b * n_mem, d)
    kv_shared = c_shared = None
    for layer in range(depth):
        mem_gain = jnp.concatenate([_tile_gain(m_k_norm_g[layer], n_mem_heads), jnp.ones((mem_w,), F32)])
        memkv = norm_matmul(memf, m_norm_g[layer], m_w_kv[layer], head_gain=mem_gain, head_tiles=(0, 1),
                            out_dtype=BF16, tn=mem_w).reshape(b, n_mem, 2 * mem_w)
        mq_gain = _tile_gain(m_q_norm_g[layer], n_mem_heads)
        if layer < n_a:
            i = layer
            n_in = 2 * lru_w + mem_w
            gain = jnp.concatenate([jnp.ones((2 * lru_w,), F32), mq_gain])
            z = norm_matmul(xf, a_norm_g[i], a_w_in[i], head_gain=gain,
                            head_tiles=(2 * lru_w // mem_w, n_in // mem_w), tn=mem_w).reshape(b, s, n_in)
            main = lru_mixer(z, a_conv_w[i], a_conv_b[i], a_gate_w[i], a_gate_b[i], a_lambda[i], c=lru_w)
            q_col_block = 2 * lru_w // mem_w
            w_out = a_w_out[i]
        else:
            if layer == n_a:
                k_gain = jnp.concatenate([_tile_gain(s_k_norm_g, n_fox_heads), jnp.ones((fox_w,), F32)])
                kv_shared = norm_matmul(xf, s_norm_g, s_w_kvf[:, :2 * fox_w], head_gain=k_gain,
                                        head_tiles=(0, fox_w // mem_w), out_dtype=BF16,
                                        tn=mem_w).reshape(b, s, 2 * fox_w)
                w_f = jnp.pad(s_w_kvf[:, 2 * fox_w:], ((0, 0), (0, LANES - n_fgate)))
                zf = norm_matmul(xf, s_norm_g, w_f, tn=LANES).reshape(b, s, LANES)
                c_shared = forget_cumsum(zf, jnp.pad(s_b_f, (0, LANES - n_fgate)))
            j = layer - n_a
            gain = jnp.concatenate([_tile_gain(b_q_norm_g[j], n_fox_heads), mq_gain])
            z = norm_matmul(xf, b_norm_g[j], b_w_in[j], head_gain=gain, head_tiles=(0, (fox_w + mem_w) // mem_w),
                            out_dtype=BF16, tn=mem_w).reshape(b, s, fox_w + mem_w)
            main = forgetting_attention(z, kv_shared, c_shared, nh=n_fox_heads)
            q_col_block = fox_w // mem_w
            w_out = b_w_out[j]
        mo = mem_attention(z, q_col_block, memkv, nh=n_mem_heads)
        xf = out_proj_residual(main.reshape(t, -1), mo.reshape(t, mem_w), w_out, xf)

        q = norm_matmul(xf, p_norm_g[layer], p_w_q[layer], out_dtype=BF16)
        idx, gates_t = peer_route(q, p_subkeys[layer])
        xf = peer_experts(idx, gates_t, xf, p_norm_g[layer], pack_expert_tables(p_u[layer], p_v[layer]))
    return xf.reshape(b, s, d)
```

```python
import functools

import jax
import jax.numpy as jnp
from jax import lax
from jax.experimental import pallas as pl
from jax.experimental.pallas import tpu as pltpu

F32 = jnp.float32
BF16 = jnp.bfloat16

HEAD_DIM = 128
LANES = 128
SUBLANES = 8
CONV_WIDTH = 4
LRU_C = 8.0
N_KEYS = 128
PEER_HEADS = 8
PEER_TOPK = 16
RMS_EPS = 1e-6
VMEM_LIMIT_BYTES = 48 * 1024 * 1024


def _params(semantics):
    return pltpu.CompilerParams(dimension_semantics=semantics, vmem_limit_bytes=VMEM_LIMIT_BYTES)


def _head_rmsnorm(y, gain):
    ms = jnp.mean(y * y, axis=-1, keepdims=True)
    return (y * lax.rsqrt(ms + RMS_EPS)) * gain


def _norm_matmul_kernel(x_ref, g_ref, w_ref, hg_ref, o_ref, xn_ref, *, head_lo, head_hi):
    j = pl.program_id(1)

    @pl.when(j == 0)
    def _():
        x = x_ref[...]
        ms = jnp.mean(x * x, axis=-1, keepdims=True)
        xn_ref[...] = ((x * lax.rsqrt(ms + RMS_EPS)) * g_ref[...]).astype(BF16)

    y = jnp.dot(xn_ref[...], w_ref[...], preferred_element_type=F32)
    tn = y.shape[1]

    def store_normed():
        for c in range(tn // HEAD_DIM):
            sl = slice(c * HEAD_DIM, (c + 1) * HEAD_DIM)
            o_ref[:, sl] = _head_rmsnorm(y[:, sl], hg_ref[:, sl]).astype(o_ref.dtype)

    def store_plain():
        o_ref[...] = y.astype(o_ref.dtype)

    if head_hi <= head_lo:
        store_plain()
    else:
        normed = jnp.logical_and(j >= head_lo, j < head_hi)
        pl.when(normed)(store_normed)
        pl.when(jnp.logical_not(normed))(store_plain)


def norm_matmul(x, g, w, *, head_gain=None, head_tiles=(0, 0), out_dtype=F32, tm=1024, tn=512):
    m, d = x.shape
    n = w.shape[1]
    tm = min(tm, m)
    tn = min(tn, n)
    assert m % tm == 0 and n % tn == 0 and tn % HEAD_DIM == 0
    if head_gain is None:
        head_gain = jnp.ones((n,), F32)
    return pl.pallas_call(
        functools.partial(_norm_matmul_kernel, head_lo=head_tiles[0], head_hi=head_tiles[1]),
        grid=(m // tm, n // tn),
        in_specs=[
            pl.BlockSpec((tm, d), lambda i, j: (i, 0)),
            pl.BlockSpec((1, d), lambda i, j: (0, 0)),
            pl.BlockSpec((d, tn), lambda i, j: (0, j)),
            pl.BlockSpec((1, tn), lambda i, j: (0, j)),
        ],
        out_specs=pl.BlockSpec((tm, tn), lambda i, j: (i, j)),
        out_shape=jax.ShapeDtypeStruct((m, n), out_dtype),
        scratch_shapes=[pltpu.VMEM((tm, d), BF16)],
        compiler_params=_params(("parallel", "arbitrary")),
        name="norm_matmul",
    )(x, g.reshape(1, d), w.astype(BF16), head_gain.reshape(1, n))


def _scan_shift(v, k, row, fill):
    return jnp.where(row >= k, pltpu.roll(v, k, 0), fill)


def _lru_kernel(x_ref, y_ref, cw_ref, cb_ref, gw_ref, gb_ref, lam_ref, o_ref,
                xe_ref, a_ref, u_ref, h_ref, *, ts, c):
    s = pl.program_id(1)

    @pl.when(s == 0)
    def _():
        xe_ref[0:SUBLANES, :] = jnp.zeros((SUBLANES, c), F32)
        h_ref[...] = jnp.zeros((SUBLANES, c), F32)

    x = x_ref[0]
    xe_ref[SUBLANES:SUBLANES + ts, :] = x
    cw = cw_ref[...]
    xc = cw[3:4, :] * x
    for k in range(CONV_WIDTH - 1):
        off = SUBLANES - (CONV_WIDTH - 1) + k
        xc = xc + cw[k:k + 1, :] * xe_ref[off:off + ts, :]
    xc = xc + cb_ref[...]
    xe_ref[0:SUBLANES, :] = x[ts - SUBLANES:ts, :]

    lam = lam_ref[...]
    softplus_neg_lam = jnp.maximum(-lam, 0.0) + jnp.log1p(jnp.exp(-jnp.abs(lam)))
    xcb = xc.astype(BF16)
    for n in range(c // HEAD_DIM):
        sl = slice(n * HEAD_DIM, (n + 1) * HEAD_DIM)
        gates = jnp.dot(xcb[:, sl], gw_ref[n], preferred_element_type=F32) + gb_ref[n]
        r = jax.nn.sigmoid(gates[:, :HEAD_DIM])
        i = jax.nn.sigmoid(gates[:, HEAD_DIM:])
        log_a = (-LRU_C * r) * softplus_neg_lam[:, sl]
        a = jnp.exp(log_a)
        a_ref[:, sl] = a
        u_ref[:, sl] = jnp.sqrt(-jnp.tanh(log_a) * (a * a + 1.0)) * (i * xc[:, sl])

    row = lax.broadcasted_iota(jnp.int32, (SUBLANES, c), 0)

    def body(g, h_prev):
        r0 = pl.multiple_of(g * SUBLANES, SUBLANES)
        a = a_ref[pl.ds(r0, SUBLANES), :]
        u = u_ref[pl.ds(r0, SUBLANES), :]
        for k in (1, 2, 4):
            u = a * _scan_shift(u, k, row, 0.0) + u
            a = a * _scan_shift(a, k, row, 1.0)
        h = u + a * h_prev
        yv = y_ref[0, pl.ds(r0, SUBLANES), :]
        o_ref[0, pl.ds(r0, SUBLANES), :] = h * jax.nn.gelu(yv)
        return jnp.broadcast_to(h[SUBLANES - 1:SUBLANES, :], (SUBLANES, c))

    h_ref[...] = lax.fori_loop(0, ts // SUBLANES, body, h_ref[...])


def lru_mixer(z, conv_w, conv_b, gate_w, gate_b, lam, *, c, ts=256):
    b, s, _ = z.shape
    ts = min(ts, s)
    nb = c // HEAD_DIM
    assert s % ts == 0 and ts % SUBLANES == 0
    return pl.pallas_call(
        functools.partial(_lru_kernel, ts=ts, c=c),
        grid=(b, s // ts),
        in_specs=[
            pl.BlockSpec((1, ts, c), lambda bi, si: (bi, si, 0)),
            pl.BlockSpec((1, ts, c), lambda bi, si: (bi, si, 1)),
            pl.BlockSpec((CONV_WIDTH, c), lambda bi, si: (0, 0)),
            pl.BlockSpec((1, c), lambda bi, si: (0, 0)),
            pl.BlockSpec((nb, HEAD_DIM, 2 * HEAD_DIM), lambda bi, si: (0, 0, 0)),
            pl.BlockSpec((nb, 1, 2 * HEAD_DIM), lambda bi, si: (0, 0, 0)),
            pl.BlockSpec((1, c), lambda bi, si: (0, 0)),
        ],
        out_specs=pl.BlockSpec((1, ts, c), lambda bi, si: (bi, si, 0)),
        out_shape=jax.ShapeDtypeStruct((b, s, c), F32),
        scratch_shapes=[
            pltpu.VMEM((ts + SUBLANES, c), F32),
            pltpu.VMEM((ts, c), F32),
            pltpu.VMEM((ts, c), F32),
            pltpu.VMEM((SUBLANES, c), F32),
        ],
        compiler_params=_params(("parallel", "arbitrary")),
        name="lru_mixer",
    )(z, z, conv_w, conv_b.reshape(1, c), gate_w.astype(BF16), gate_b.reshape(nb, 1, 2 * HEAD_DIM),
      lam.reshape(1, c))


def _mem_attn_kernel(q_ref, kv_ref, o_ref, *, nh):
    q = q_ref[0]
    for h in range(nh):
        sl = slice(h * HEAD_DIM, (h + 1) * HEAD_DIM)
        qh = q[:, sl].astype(BF16)
        kh = kv_ref[0, :, sl]
        vh = kv_ref[0, :, nh * HEAD_DIM + h * HEAD_DIM:nh * HEAD_DIM + (h + 1) * HEAD_DIM]
        s = lax.dot_general(qh, kh, (((1,), (1,)), ((), ())), preferred_element_type=F32) * (HEAD_DIM ** -0.5)
        m = jnp.max(s, axis=-1, keepdims=True)
        p = jnp.exp(s - m)
        p = p / jnp.sum(p, axis=-1, keepdims=True)
        o_ref[0, :, sl] = jnp.dot(p.astype(BF16), vh, preferred_element_type=F32).astype(o_ref.dtype)


def mem_attention(z, q_col_block, memkv, *, nh, tq=512):
    b, s, _ = z.shape
    nm = memkv.shape[1]
    w = nh * HEAD_DIM
    tq = min(tq, s)
    return pl.pallas_call(
        functools.partial(_mem_attn_kernel, nh=nh),
        grid=(b, s // tq),
        in_specs=[
            pl.BlockSpec((1, tq, w), lambda bi, qi: (bi, qi, q_col_block)),
            pl.BlockSpec((1, nm, 2 * w), lambda bi, qi: (bi, 0, 0)),
        ],
        out_specs=pl.BlockSpec((1, tq, w), lambda bi, qi: (bi, qi, 0)),
        out_shape=jax.ShapeDtypeStruct((b, s, w), BF16),
        compiler_params=_params(("parallel", "parallel")),
        name="mem_attention",
    )(z, memkv)


def _out_proj_kernel(main_ref, mo_ref, w1_ref, w2_ref, x_ref, o_ref):
    acc = jnp.dot(main_ref[...].astype(BF16), w1_ref[...], preferred_element_type=F32)
    acc = acc + jnp.dot(mo_ref[...].astype(BF16), w2_ref[...], preferred_element_type=F32)
    o_ref[...] = x_ref[...] + acc


def out_proj_residual(main, mo, w_out, x, *, tm=1024, tn=512):
    m, d = x.shape
    k1, k2 = main.shape[1], mo.shape[1]
    tm = min(tm, m)
    w = w_out.astype(BF16)
    return pl.pallas_call(
        _out_proj_kernel,
        grid=(m // tm, d // tn),
        in_specs=[
            pl.BlockSpec((tm, k1), lambda i, j: (i, 0)),
            pl.BlockSpec((tm, k2), lambda i, j: (i, 0)),
            pl.BlockSpec((k1, tn), lambda i, j: (0, j)),
            pl.BlockSpec((k2, tn), lambda i, j: (0, j)),
            pl.BlockSpec((tm, tn), lambda i, j: (i, j)),
        ],
        out_specs=pl.BlockSpec((tm, tn), lambda i, j: (i, j)),
        out_shape=jax.ShapeDtypeStruct((m, d), F32),
        compiler_params=_params(("parallel", "parallel")),
        name="out_proj_residual",
    )(main, mo, w[:k1], w[k1:], x)


def _forget_cumsum_kernel(zf_ref, bf_ref, c_ref, carry_ref, *, ts):
    @pl.when(pl.program_id(1) == 0)
    def _():
        carry_ref[...] = jnp.zeros((SUBLANES, LANES), F32)

    row = lax.broadcasted_iota(jnp.int32, (SUBLANES, LANES), 0)

    def body(g, carry):
        r0 = pl.multiple_of(g * SUBLANES, SUBLANES)
        v = zf_ref[0, pl.ds(r0, SUBLANES), :] + bf_ref[...]
        lf = jnp.minimum(v, 0.0) - jnp.log1p(jnp.exp(-jnp.abs(v)))
        for k in (1, 2, 4):
            lf = lf + _scan_shift(lf, k, row, 0.0)
        cg = lf + carry
        c_ref[0, pl.ds(r0, SUBLANES), :] = cg
        return jnp.broadcast_to(cg[SUBLANES - 1:SUBLANES, :], (SUBLANES, LANES))

    carry_ref[...] = lax.fori_loop(0, ts // SUBLANES, body, carry_ref[...])


def forget_cumsum(zf, b_f, *, ts=512):
    b, s, w = zf.shape
    ts = min(ts, s)
    return pl.pallas_call(
        functools.partial(_forget_cumsum_kernel, ts=ts),
        grid=(b, s // ts),
        in_specs=[
            pl.BlockSpec((1, ts, w), lambda bi, si: (bi, si, 0)),
            pl.BlockSpec((1, w), lambda bi, si: (0, 0)),
        ],
        out_specs=pl.BlockSpec((1, ts, w), lambda bi, si: (bi, si, 0)),
        out_shape=jax.ShapeDtypeStruct((b, s, w), F32),
        scratch_shapes=[pltpu.VMEM((SUBLANES, LANES), F32)],
        compiler_params=_params(("parallel", "arbitrary")),
        name="forget_cumsum",
    )(zf, b_f.reshape(1, w))


def _fox_kernel(q_ref, k_ref, v_ref, cq_ref, ck_ref, o_ref, m_ref, l_ref, acc_ref, *, tq, tk):
    qi = pl.program_id(2)
    log2e = 1.4426950408889634
    q = q_ref[0]
    cq2 = jnp.broadcast_to(cq_ref[0, 0] * log2e, (tq, LANES))
    m_ref[...] = jnp.full((tq, LANES), -jnp.inf, F32)
    l_ref[...] = jnp.zeros((tq, LANES), F32)
    acc_ref[...] = jnp.zeros((tq, HEAD_DIM), F32)
    n_tiles = tk // LANES

    def logits2(kb):
        k0 = pl.multiple_of(kb * tk, tk)
        s = lax.dot_general(q, k_ref[0, pl.ds(k0, tk), :], (((1,), (1,)), ((), ())), preferred_element_type=F32)
        ck2 = ck_ref[0, 0, kb] * log2e
        return [s[:, j * LANES:(j + 1) * LANES] * (HEAD_DIM ** -0.5 * log2e) + cq2 - ck2[:, j * LANES:(j + 1) * LANES]
                for j in range(n_tiles)], k0

    def update(tiles, k0):
        m_prev = m_ref[...]
        tile_max = functools.reduce(jnp.maximum, tiles)
        m_new = jnp.maximum(m_prev, jnp.max(tile_max, axis=-1, keepdims=True))
        alpha = jnp.exp2(m_prev - m_new)
        ps = [jnp.exp2(t - m_new) for t in tiles]
        l_ref[...] = alpha * l_ref[...] + functools.reduce(jnp.add, ps)
        p = jnp.concatenate(ps, axis=1).astype(BF16)
        acc_ref[...] = alpha * acc_ref[...] + jnp.dot(p, v_ref[0, pl.ds(k0, tk), :], preferred_element_type=F32)
        m_ref[...] = m_new

    n_diag = tq // tk
    n_below = qi * n_diag

    def body(kb, carry):
        update(*logits2(kb))
        return carry

    lax.fori_loop(0, n_below, body, 0)
    q_pos = lax.broadcasted_iota(jnp.int32, (tq, LANES), 0)
    k_pos = lax.broadcasted_iota(jnp.int32, (tq, LANES), 1)
    for jb in range(n_diag):
        tiles, k0 = logits2(n_below + jb)
        update([jnp.where(k_pos + (jb * tk + j * LANES) <= q_pos, t, -jnp.inf) for j, t in enumerate(tiles)], k0)
    o_ref[0] = (acc_ref[...] / jnp.sum(l_ref[...], axis=-1, keepdims=True)).astype(o_ref.dtype)


def forgetting_attention(zq, kv, c, *, nh, tq=512, tk=512):
    b, s, _ = zq.shape
    tq = min(tq, s)
    tk = min(tk, tq)
    assert s % tq == 0 and tq % tk == 0
    ch = jnp.transpose(c[:, :, :nh], (0, 2, 1))
    cq = ch[:, :, :, None]
    ck = ch.reshape(b, nh, s // tk, 1, tk)
    return pl.pallas_call(
        functools.partial(_fox_kernel, tq=tq, tk=tk),
        grid=(b, nh, s // tq),
        in_specs=[
            pl.BlockSpec((1, tq, HEAD_DIM), lambda bi, h, qi: (bi, qi, h)),
            pl.BlockSpec((1, s, HEAD_DIM), lambda bi, h, qi: (bi, 0, h)),
            pl.BlockSpec((1, s, HEAD_DIM), lambda bi, h, qi: (bi, 0, nh + h)),
            pl.BlockSpec((1, 1, tq, 1), lambda bi, h, qi: (bi, h, qi, 0)),
            pl.BlockSpec((1, 1, s // tk, 1, tk), lambda bi, h, qi: (bi, h, 0, 0, 0)),
        ],
        out_specs=pl.BlockSpec((1, tq, HEAD_DIM), lambda bi, h, qi: (bi, qi, h)),
        out_shape=jax.ShapeDtypeStruct((b, s, nh * HEAD_DIM), BF16),
        scratch_shapes=[
            pltpu.VMEM((tq, LANES), F32),
            pltpu.VMEM((tq, LANES), F32),
            pltpu.VMEM((tq, HEAD_DIM), F32),
        ],
        compiler_params=_params(("parallel", "parallel", "arbitrary")),
        name="forgetting_attention",
    )(zq, kv, kv, cq, ck)


def _topk_rows(s, iota, k, payload=None):
    big = float(s.shape[0])
    vals, picks = [], []
    for _ in range(k):
        m = jnp.max(s, axis=0, keepdims=True)
        pos = jnp.min(jnp.where(s == m, iota, big), axis=0, keepdims=True)
        sel = iota == pos
        if payload is None:
            picks.append(pos)
        else:
            picks.append(jnp.max(jnp.where(sel, payload, -1.0), axis=0, keepdims=True))
        vals.append(m)
        s = jnp.where(sel, -jnp.inf, s)
    return vals, picks


def _route_kernel(q_ref, sk_ref, idx_ref, gt_ref, *, tm):
    iota_k = lax.broadcasted_iota(jnp.int32, (N_KEYS, tm), 0).astype(F32)
    half_k = PEER_TOPK // 2
    n_cand = PEER_TOPK + (half_k - 1) * half_k + half_k
    iota_c = lax.broadcasted_iota(jnp.int32, (n_cand, tm), 0).astype(F32)
    idx_rows, gate_rows = [], []
    for h in range(PEER_HEADS):
        halves = []
        for p in range(2):
            col = (2 * h + p) * HEAD_DIM
            st = lax.dot_general(sk_ref[h, p], q_ref[:, col:col + HEAD_DIM], (((1,), (1,)), ((), ())),
                                 preferred_element_type=F32)
            halves.append(_topk_rows(st, iota_k, PEER_TOPK))
        (v1, i1), (v2, i2) = halves
        s2 = jnp.concatenate(v2, axis=0)
        e2 = jnp.concatenate(i2, axis=0)
        s1_tail = jnp.concatenate(v1[half_k:], axis=0)
        e1_tail = jnp.concatenate(i1[half_k:], axis=0)
        cand = jnp.concatenate([v1[0] + s2] + [v1[a] + s2[:half_k] for a in range(1, half_k)]
                               + [s1_tail + v2[0]], axis=0)
        cand_e = jnp.concatenate([i1[0] * float(N_KEYS) + e2]
                                 + [i1[a] * float(N_KEYS) + e2[:half_k] for a in range(1, half_k)]
                                 + [e1_tail * float(N_KEYS) + i2[0]], axis=0)
        best, experts = _topk_rows(cand, iota_c, PEER_TOPK, payload=cand_e)
        bs = jnp.concatenate(best, axis=0)
        ex = jnp.exp(bs - bs[0:1, :])
        gate_rows.append(ex / jnp.sum(ex, axis=0, keepdims=True))
        idx_rows.append(jnp.concatenate(experts, axis=0))
    gt_ref[...] = jnp.concatenate(gate_rows, axis=0)
    idx_ref[...] = jnp.transpose(jnp.concatenate(idx_rows, axis=0)).astype(jnp.int32)


def peer_route(q, subkeys, *, tm=256):
    t, dq = q.shape
    tm = min(tm, t)
    nsel = PEER_HEADS * PEER_TOPK
    return pl.pallas_call(
        functools.partial(_route_kernel, tm=tm),
        grid=(t // tm,),
        in_specs=[
            pl.BlockSpec((tm, dq), lambda i: (i, 0)),
            pl.BlockSpec((PEER_HEADS, 2, N_KEYS, HEAD_DIM), lambda i: (0, 0, 0, 0)),
        ],
        out_specs=[
            pl.BlockSpec((tm, nsel), lambda i: (i, 0)),
            pl.BlockSpec((nsel, tm), lambda i: (0, i)),
        ],
        out_shape=[jax.ShapeDtypeStruct((t, nsel), jnp.int32), jax.ShapeDtypeStruct((nsel, t), F32)],
        compiler_params=_params(("parallel",)),
        name="peer_route",
    )(q, subkeys.astype(BF16))


def _pack_bf16_pairs(w):
    half = w.shape[1] // 2
    bits = lax.bitcast_convert_type(w.astype(BF16).astype(F32), jnp.uint32)
    return (bits[:, :half] >> 16) | (bits[:, half:] & jnp.uint32(0xFFFF0000))


def _pack_kernel(u_ref, v_ref, o_ref):
    half = u_ref.shape[1] // 2
    o_ref[:, 0, 0:half] = _pack_bf16_pairs(u_ref[...])
    o_ref[:, 0, half:2 * half] = _pack_bf16_pairs(v_ref[...])


def pack_expert_tables(p_u, p_v, layer, *, tb=256):
    _, e, d = p_u.shape
    return pl.pallas_call(
        _pack_kernel,
        grid=(e // tb,),
        in_specs=[
            pl.BlockSpec((None, tb, d), lambda i: (layer, i, 0)),
            pl.BlockSpec((None, tb, d), lambda i: (layer, i, 0)),
        ],
        out_specs=pl.BlockSpec((tb, 1, d), lambda i: (i, 0, 0)),
        out_shape=jax.ShapeDtypeStruct((e, 1, d), jnp.uint32),
        compiler_params=_params(("parallel",)),
        name="pack_expert_tables",
    )(p_u, p_v)


def _unpack_lo(w):
    return lax.bitcast_convert_type(w << 16, F32)


def _unpack_hi(w):
    return lax.bitcast_convert_type(w & jnp.uint32(0xFFFF0000), F32)


def _expert_kernel(idx_ref, gt_ref, x_ref, g_ref, tab_ref, o_ref, hn_ref, buf, sem, *, tb, nsel, nslot, d):
    half = d // 2
    ntile = half // LANES

    def issue(t, slot):
        for j in range(nsel):
            pltpu.make_async_copy(tab_ref.at[idx_ref[t, j]],
                                  buf.at[slot, pl.ds(j, 1), :], sem.at[slot]).start(priority=j % 2)

    def wait(slot):
        pltpu.make_async_copy(buf.at[slot], buf.at[slot], sem.at[slot]).wait()

    for t0 in range(nslot - 1):
        issue(t0, t0)

    hn_ref[...] = _head_rmsnorm(x_ref[...], g_ref[...])
    lane = lax.broadcasted_iota(jnp.int32, (nsel, tb), 1)

    def body(g, carry):
        base = pl.multiple_of(g * SUBLANES, SUBLANES)
        hn8 = hn_ref[pl.ds(base, SUBLANES), :]
        rows = []
        for r in range(SUBLANES):
            t = base + r
            nxt = t + (nslot - 1)
            nxt_slot = (r + nslot - 1) % nslot
            if r + nslot - 1 < SUBLANES:
                issue(nxt, nxt_slot)
            else:
                pl.when(nxt < tb)(functools.partial(issue, nxt, nxt_slot))
            slot = r % nslot
            wait(slot)
            gate = jnp.sum(jnp.where(lane == t, gt_ref[...], 0.0), axis=-1, keepdims=True)
            acc = None
            for k in range(ntile):
                words = buf[slot, :, k * LANES:(k + 1) * LANES]
                part = (_unpack_lo(words) * hn8[r:r + 1, k * LANES:(k + 1) * LANES]
                        + _unpack_hi(words) * hn8[r:r + 1, half + k * LANES:half + (k + 1) * LANES])
                acc = part if acc is None else acc + part
            act = jnp.sum(acc, axis=-1, keepdims=True)
            w = gate * jax.nn.gelu(act)
            lo_tiles, hi_tiles = [], []
            for k in range(ntile):
                words = buf[slot, :, half + k * LANES:half + (k + 1) * LANES]
                lo_tiles.append(jnp.sum(_unpack_lo(words) * w, axis=0, keepdims=True))
                hi_tiles.append(jnp.sum(_unpack_hi(words) * w, axis=0, keepdims=True))
            rows.append(jnp.concatenate(lo_tiles + hi_tiles, axis=1))
        o_ref[pl.ds(base, SUBLANES), :] = x_ref[pl.ds(base, SUBLANES), :] + jnp.concatenate(rows, axis=0)
        return carry

    lax.fori_loop(0, tb // SUBLANES, body, 0)


def peer_experts(idx, gates_t, x, norm_g, table, *, tb=128, nslot=4):
    t, d = x.shape
    nsel = idx.shape[1]
    tb = min(tb, t)
    assert t % tb == 0 and tb % SUBLANES == 0 and SUBLANES % nslot == 0
    return pl.pallas_call(
        functools.partial(_expert_kernel, tb=tb, nsel=nsel, nslot=nslot, d=d),
        grid=(t // tb,),
        in_specs=[
            pl.BlockSpec((tb, nsel), lambda i: (i, 0), memory_space=pltpu.SMEM),
            pl.BlockSpec((nsel, tb), lambda i: (0, i)),
            pl.BlockSpec((tb, d), lambda i: (i, 0)),
            pl.BlockSpec((1, d), lambda i: (0, 0)),
            pl.BlockSpec(memory_space=pl.ANY),
        ],
        out_specs=pl.BlockSpec((tb, d), lambda i: (i, 0)),
        out_shape=jax.ShapeDtypeStruct((t, d), F32),
        scratch_shapes=[
            pltpu.VMEM((tb, d), F32),
            pltpu.VMEM((nslot, nsel, d), jnp.uint32),
            pltpu.SemaphoreType.DMA((nslot,)),
        ],
        compiler_params=_params(("arbitrary",)),
        name="peer_experts",
    )(idx, gates_t, x, norm_g.reshape(1, d), table)


def _tile_gain(g, n):
    return jnp.tile(g, n)


def kernel(x, mem, a_norm_g, a_w_in, a_conv_w, a_conv_b, a_gate_w, a_gate_b, a_lambda, a_w_out, s_norm_g, s_w_kvf, s_b_f, s_k_norm_g, b_norm_g, b_w_in, b_q_norm_g, b_w_out, m_norm_g, m_w_kv, m_q_norm_g, m_k_norm_g, p_norm_g, p_w_q, p_subkeys, p_u, p_v):
    b, s, d = x.shape
    t = b * s
    n_mem = mem.shape[1]
    depth = m_norm_g.shape[0]
    n_a = a_norm_g.shape[0]
    lru_w = a_conv_w.shape[-1]
    mem_w = m_w_kv.shape[-1] // 2
    fox_w = b_w_in.shape[-1] - mem_w
    n_mem_heads = mem_w // HEAD_DIM
    n_fox_heads = fox_w // HEAD_DIM
    n_fgate = s_w_kvf.shape[1] - 2 * fox_w

    xf = x.reshape(t, d)
    memf = mem.reshape(b * n_mem, d)
    kv_shared = c_shared = None
    for layer in range(depth):
        mem_gain = jnp.concatenate([_tile_gain(m_k_norm_g[layer], n_mem_heads), jnp.ones((mem_w,), F32)])
        memkv = norm_matmul(memf, m_norm_g[layer], m_w_kv[layer], head_gain=mem_gain, head_tiles=(0, 1),
                            out_dtype=BF16, tn=mem_w).reshape(b, n_mem, 2 * mem_w)
        mq_gain = _tile_gain(m_q_norm_g[layer], n_mem_heads)
        if layer < n_a:
            i = layer
            n_in = 2 * lru_w + mem_w
            gain = jnp.concatenate([jnp.ones((2 * lru_w,), F32), mq_gain])
            z = norm_matmul(xf, a_norm_g[i], a_w_in[i], head_gain=gain,
                            head_tiles=(2 * lru_w // mem_w, n_in // mem_w), tn=mem_w).reshape(b, s, n_in)
            main = lru_mixer(z, a_conv_w[i], a_conv_b[i], a_gate_w[i], a_gate_b[i], a_lambda[i], c=lru_w)
            q_col_block = 2 * lru_w // mem_w
            w_out = a_w_out[i]
        else:
            if layer == n_a:
                k_gain = jnp.concatenate([_tile_gain(s_k_norm_g, n_fox_heads), jnp.ones((fox_w,), F32)])
                kv_shared = norm_matmul(xf, s_norm_g, s_w_kvf[:, :2 * fox_w], head_gain=k_gain,
                                        head_tiles=(0, fox_w // mem_w), out_dtype=BF16,
                                        tn=mem_w).reshape(b, s, 2 * fox_w)
                w_f = jnp.pad(s_w_kvf[:, 2 * fox_w:], ((0, 0), (0, LANES - n_fgate)))
                zf = norm_matmul(xf, s_norm_g, w_f, tn=LANES).reshape(b, s, LANES)
                c_shared = forget_cumsum(zf, jnp.pad(s_b_f, (0, LANES - n_fgate)))
            j = layer - n_a
            gain = jnp.concatenate([_tile_gain(b_q_norm_g[j], n_fox_heads), mq_gain])
            z = norm_matmul(xf, b_norm_g[j], b_w_in[j], head_gain=gain, head_tiles=(0, (fox_w + mem_w) // mem_w),
                            out_dtype=BF16, tn=mem_w).reshape(b, s, fox_w + mem_w)
            main = forgetting_attention(z, kv_shared, c_shared, nh=n_fox_heads)
            q_col_block = fox_w // mem_w
            w_out = b_w_out[j]
        mo = mem_attention(z, q_col_block, memkv, nh=n_mem_heads)
        xf = out_proj_residual(main.reshape(t, -1), mo.reshape(t, mem_w), w_out, xf)

        q = norm_matmul(xf, p_norm_g[layer], p_w_q[layer], out_dtype=BF16)
        idx, gates_t = peer_route(q, p_subkeys[layer])
        xf = peer_experts(idx, gates_t, xf, p_norm_g[layer], pack_expert_tables(p_u, p_v, layer))
    return xf.reshape(b, s, d)
```
